```python
import jax, jax.numpy as jnp
from jax import lax
import numpy as np

D_MODEL = 2048
BATCH = 8
SEQ = 4096
DEPTH = 4

N_MIXERS = 3
MEM_LEN = 256
EPS = 1e-6
BLOCK = 128

SSD_EXPAND = 2
SSD_D_INNER = SSD_EXPAND * D_MODEL
SSD_HEAD_DIM = 64
SSD_HEADS = SSD_D_INNER // SSD_HEAD_DIM
SSD_GROUPS = 8
SSD_HEADS_PER_GROUP = SSD_HEADS // SSD_GROUPS
SSD_STATE = 128
SSD_CONV = 4
SSD_CHUNK = 128
SSD_CONV_DIM = SSD_D_INNER + 2 * SSD_GROUPS * SSD_STATE
SSD_IN_DIM = 2 * SSD_D_INNER + 2 * SSD_GROUPS * SSD_STATE + SSD_HEADS

SG_WIDTH = 2 * D_MODEL
SG_GROUPS = 16
SG_CHUNK = 128

SB_HEAD_DIM = 128
SB_HEADS = D_MODEL // SB_HEAD_DIM

XA_HEADS = 4
XA_HEAD_DIM = 128
XA_WIDTH = XA_HEADS * XA_HEAD_DIM

FFN_DIM = 5632
FFN_CONV = 3

N_SSD = (DEPTH + N_MIXERS - 1) // N_MIXERS
N_SG = (DEPTH + N_MIXERS - 2) // N_MIXERS
N_SB = DEPTH // N_MIXERS

kernel_name = "interleaved_ssd_gmlp_stickbreak_trunk"


def rmsnorm(x, g):
    xf = x.astype(jnp.float32)
    y = xf * lax.rsqrt(jnp.mean(xf * xf, axis=-1, keepdims=True) + EPS)
    return (y * g.astype(jnp.float32)).astype(x.dtype)


def layernorm(x, g, b):
    xf = x.astype(jnp.float32)
    mu = jnp.mean(xf, axis=-1, keepdims=True)
    xc = xf - mu
    y = xc * lax.rsqrt(jnp.mean(xc * xc, axis=-1, keepdims=True) + EPS)
    return (y * g.astype(jnp.float32) + b.astype(jnp.float32)).astype(x.dtype)


def causal_dwconv(x, w, b):
    K, C = w.shape
    y = lax.conv_general_dilated(
        x, w[:, None, :].astype(x.dtype), window_strides=(1,), padding=[(K - 1, 0)],
        dimension_numbers=("NWC", "WIO", "NWC"), feature_group_count=C)
    return y + b.astype(x.dtype)


def ssd_mixer(h, w_in, conv_w, conv_b, dt_bias, a_log, d_skip, norm_g, w_out):
    Bsz, L, _ = h.shape
    G, R, P, N, Q = SSD_GROUPS, SSD_HEADS_PER_GROUP, SSD_HEAD_DIM, SSD_STATE, SSD_CHUNK
    nc = L // Q
    f32 = jnp.float32
    proj = h @ w_in
    z, xbc, dt = jnp.split(proj, [SSD_D_INNER, SSD_D_INNER + SSD_CONV_DIM], axis=-1)
    xbc = jax.nn.silu(causal_dwconv(xbc, conv_w, conv_b))
    xs, Bm, Cm = jnp.split(xbc, [SSD_D_INNER, SSD_D_INNER + G * N], axis=-1)
    dt = jax.nn.softplus(dt.astype(f32) + dt_bias.astype(f32))
    A = -jnp.exp(a_log.astype(f32)).reshape(G, R)
    xs_f = xs.astype(f32).reshape(Bsz, nc, Q, G, R, P)
    dtc = dt.reshape(Bsz, nc, Q, G, R)
    Bc = Bm.astype(f32).reshape(Bsz, nc, Q, G, N)
    Cc = Cm.astype(f32).reshape(Bsz, nc, Q, G, N)
    a_cum = jnp.cumsum(dtc * A, axis=2)
    xdt = xs_f * dtc[..., None]
    causal = jnp.tril(jnp.ones((Q, Q), dtype=bool))[None, None, :, :, None, None]
    seg = a_cum[:, :, :, None] - a_cum[:, :, None, :]
    Lmat = jnp.exp(jnp.where(causal, seg, -jnp.inf))
    CB = jnp.einsum("bclgn,bcsgn->bclsg", Cc, Bc)
    y_diag = jnp.einsum("bclsg,bclsgr,bcsgrp->bclgrp", CB, Lmat, xdt)
    decay_states = jnp.exp(a_cum[:, :, -1:] - a_cum)
    states = jnp.einsum("bclgn,bclgr,bclgrp->bcgrpn", Bc, decay_states, xdt)
    chunk_decay = jnp.exp(a_cum[:, :, -1])

    def step(carry, inp):
        st, dec = inp
        return carry * dec[..., None, None] + st, carry

    init = jnp.zeros((Bsz, G, R, P, N), f32)
    _, prev = lax.scan(step, init, (jnp.moveaxis(states, 1, 0), jnp.moveaxis(chunk_decay, 1, 0)))
    prev = jnp.moveaxis(prev, 0, 1)
    y_off = jnp.einsum("bclgn,bcgrpn,bclgr->bclgrp", Cc, prev, jnp.exp(a_cum))
    y = y_diag + y_off + xs_f * d_skip.astype(f32).reshape(G, R)[..., None]
    y = y.reshape(Bsz, L, SSD_D_INNER) * jax.nn.silu(z.astype(f32))
    y = rmsnorm(y, norm_g).astype(h.dtype)
    return y @ w_out


def sgu_mixer(h, w_in, v_norm_g, v_norm_b, w_spatial, b_spatial, w_out):
    Bsz, L, _ = h.shape
    nc = L // SG_CHUNK
    uv = jax.nn.gelu(h @ w_in)
    u, v = jnp.split(uv, 2, axis=-1)
    v = layernorm(v, v_norm_g, v_norm_b)
    vc = v.reshape(Bsz, nc, SG_CHUNK, SG_GROUPS, SG_WIDTH // SG_GROUPS)
    mask = jnp.tril(jnp.ones((SG_CHUNK, SG_CHUNK), dtype=w_spatial.dtype))
    ws = w_spatial * mask
    mixed = jnp.einsum("gts,bcsgd->bctgd", ws, vc) + b_spatial.T[None, None, :, :, None]
    return (u * mixed.reshape(Bsz, L, SG_WIDTH)) @ w_out


def stick_breaking_mixer(h, w_qkv, w_out):
    Bsz, L, _ = h.shape
    f32 = jnp.float32
    qkv = (h @ w_qkv).reshape(Bsz, L, 3, SB_HEADS, SB_HEAD_DIM)
    q, k, v = qkv[:, :, 0], qkv[:, :, 1], qkv[:, :, 2]
    scale = SB_HEAD_DIM ** -0.5
    outs = []
    for i in range(L // BLOCK):
        q0 = i * BLOCK
        kend = q0 + BLOCK
        qb, kb, vb = q[:, q0:kend], k[:, :kend], v[:, :kend]
        z = jnp.einsum("bthd,bshd->bhts", qb, kb).astype(f32) * scale
        t_idx = q0 + jnp.arange(BLOCK)
        s_idx = jnp.arange(kend)
        valid = s_idx[None, :] < t_idx[:, None]
        log_beta = jax.nn.log_sigmoid(z)
        log_1mb = jnp.where(valid, jax.nn.log_sigmoid(-z), 0.0)
        tail = lax.cumsum(log_1mb, axis=3, reverse=True) - log_1mb
        A = jnp.where(valid, jnp.exp(log_beta + tail), 0.0)
        outs.append(jnp.einsum("bhts,bshd->bthd", A.astype(vb.dtype), vb))
    o = jnp.concatenate(outs, axis=1).reshape(Bsz, L, D_MODEL)
    return o @ w_out


def memory_cross_attention(h, mem_n, wq, wkv, wo):
    Bsz, L, _ = h.shape
    M = mem_n.shape[1]
    q = (h @ wq).reshape(Bsz, L, XA_HEADS, XA_HEAD_DIM)
    kv = (mem_n @ wkv).reshape(Bsz, M, 2, XA_HEADS, XA_HEAD_DIM)
    k, v = kv[:, :, 0], kv[:, :, 1]
    s = jnp.einsum("bthd,bmhd->bhtm", q, k).astype(jnp.float32) * (XA_HEAD_DIM ** -0.5)
    p = jax.nn.softmax(s, axis=-1).astype(v.dtype)
    o = jnp.einsum("bhtm,bmhd->bthd", p, v).reshape(Bsz, L, XA_WIDTH)
    return o @ wo


def conv_gated_ffn(h, w_in, conv_w, conv_b, w_out):
    gu = causal_dwconv(h @ w_in, conv_w, conv_b)
    g, u = jnp.split(gu, 2, axis=-1)
    return (jax.nn.gelu(g, approximate=True) * u) @ w_out


def setup_inputs(seed: int = 0) -> dict:
    key = jax.random.key(seed)
    keys = iter(jax.random.split(key, 64))
    f32 = jnp.float32

    def nrm(shape, scale):
        return jax.random.normal(next(keys), shape, f32) * scale

    def gain(shape):
        return 1.0 + nrm(shape, 0.05)

    D = D_MODEL
    d = {}
    d["x"] = nrm((BATCH, SEQ, D), 1.0)
    d["mem"] = nrm((BATCH, MEM_LEN, D), 1.0)
    for name in ["ln_mix_pre", "ln_mix_post", "ln_mem", "ln_xa_pre", "ln_xa_post", "ln_ffn_pre", "ln_ffn_post"]:
        d[name] = gain((DEPTH, D))
    d["xa_wq"] = nrm((DEPTH, D, XA_WIDTH), D ** -0.5)
    d["xa_wkv"] = nrm((DEPTH, D, 2 * XA_WIDTH), D ** -0.5)
    d["xa_wo"] = nrm((DEPTH, XA_WIDTH, D), XA_WIDTH ** -0.5)
    d["ffn_w_in"] = nrm((DEPTH, D, 2 * FFN_DIM), D ** -0.5)
    d["ffn_conv_w"] = nrm((DEPTH, FFN_CONV, 2 * FFN_DIM), FFN_CONV ** -0.5)
    d["ffn_conv_b"] = nrm((DEPTH, 2 * FFN_DIM), 0.02)
    d["ffn_w_out"] = nrm((DEPTH, FFN_DIM, D), FFN_DIM ** -0.5)
    d["ssd_w_in"] = nrm((N_SSD, D, SSD_IN_DIM), D ** -0.5)
    d["ssd_conv_w"] = nrm((N_SSD, SSD_CONV, SSD_CONV_DIM), SSD_CONV ** -0.5)
    d["ssd_conv_b"] = nrm((N_SSD, SSD_CONV_DIM), 0.02)
    dt0 = jnp.exp(jax.random.uniform(next(keys), (N_SSD, SSD_HEADS), f32,
                                     minval=math_log(1e-3), maxval=math_log(1e-1)))
    d["ssd_dt_bias"] = dt0 + jnp.log(-jnp.expm1(-dt0))
    d["ssd_a_log"] = jnp.log(jax.random.uniform(next(keys), (N_SSD, SSD_HEADS), f32, minval=1.0, maxval=16.0))
    d["ssd_d"] = gain((N_SSD, SSD_HEADS))
    d["ssd_norm"] = gain((N_SSD, SSD_D_INNER))
    d["ssd_w_out"] = nrm((N_SSD, SSD_D_INNER, D), SSD_D_INNER ** -0.5)
    d["sg_w_in"] = nrm((N_SG, D, 2 * SG_WIDTH), D ** -0.5)
    d["sg_v_norm_g"] = gain((N_SG, SG_WIDTH))
    d["sg_v_norm_b"] = nrm((N_SG, SG_WIDTH), 0.02)
    d["sg_w_spatial"] = nrm((N_SG, SG_GROUPS, SG_CHUNK, SG_CHUNK), 0.5 * SG_CHUNK ** -0.5)
    d["sg_b_spatial"] = 1.0 + nrm((N_SG, SG_GROUPS, SG_CHUNK), 0.1)
    d["sg_w_out"] = nrm((N_SG, SG_WIDTH, D), SG_WIDTH ** -0.5)
    d["sb_w_qkv"] = nrm((N_SB, D, 3 * D), D ** -0.5)
    d["sb_w_out"] = nrm((N_SB, D, D), D ** -0.5)
    return d


def math_log(v):
    return float(np.log(v))


def reference(x, mem, ln_mix_pre, ln_mix_post, ln_mem, ln_xa_pre, ln_xa_post, ln_ffn_pre, ln_ffn_post,
              xa_wq, xa_wkv, xa_wo, ffn_w_in, ffn_conv_w, ffn_conv_b, ffn_w_out,
              ssd_w_in, ssd_conv_w, ssd_conv_b, ssd_dt_bias, ssd_a_log, ssd_d, ssd_norm, ssd_w_out,
              sg_w_in, sg_v_norm_g, sg_v_norm_b, sg_w_spatial, sg_b_spatial, sg_w_out,
              sb_w_qkv, sb_w_out):
    for i in range(DEPTH):
        kind = i % N_MIXERS
        j = i // N_MIXERS
        hn = rmsnorm(x, ln_mix_pre[i])
        if kind == 0:
            m = ssd_mixer(hn, ssd_w_in[j], ssd_conv_w[j], ssd_conv_b[j], ssd_dt_bias[j],
                          ssd_a_log[j], ssd_d[j], ssd_norm[j], ssd_w_out[j])
        elif kind == 1:
            m = sgu_mixer(hn, sg_w_in[j], sg_v_norm_g[j], sg_v_norm_b[j],
                          sg_w_spatial[j], sg_b_spatial[j], sg_w_out[j])
        else:
            m = stick_breaking_mixer(hn, sb_w_qkv[j], sb_w_out[j])
        x = x + rmsnorm(m, ln_mix_post[i])
        mem_n = rmsnorm(mem, ln_mem[i])
        c = memory_cross_attention(rmsnorm(x, ln_xa_pre[i]), mem_n, xa_wq[i], xa_wkv[i], xa_wo[i])
        x = x + rmsnorm(c, ln_xa_post[i])
        f = conv_gated_ffn(rmsnorm(x, ln_ffn_pre[i]), ffn_w_in[i], ffn_conv_w[i], ffn_conv_b[i], ffn_w_out[i])
        x = x + rmsnorm(f, ln_ffn_post[i])
    return x
```

```python
import functools

import jax
import jax.numpy as jnp
from jax import lax
from jax.experimental import pallas as pl
from jax.experimental.pallas import tpu as pltpu

F32 = jnp.float32
BF16 = jnp.bfloat16

EPS = 1e-6
LANES = 128
HALO = 16
VMEM_LIMIT_BYTES = 56 * 1024 * 1024

SSD_HEAD_DIM = 64
SSD_GROUPS = 8
SSD_STATE = 128
SSD_CHUNK = 128
SG_GROUPS = 16
SG_CHUNK = 128
SB_HEAD_DIM = 128
XA_HEADS = 4
XA_HEAD_DIM = 128

ROW_CHUNK = 256


def _params(*sem):
    return pltpu.CompilerParams(dimension_semantics=sem, vmem_limit_bytes=VMEM_LIMIT_BYTES)


def _pick(n, prefs):
    for p in prefs:
        if n % p == 0:
            return p
    raise ValueError(f"no tile in {prefs} divides {n}")


def _rms(x, g):
    ms = jnp.mean(x * x, axis=-1, keepdims=True)
    return x * lax.rsqrt(ms + EPS) * g


def _softplus(x):
    return jnp.maximum(x, 0.0) + jnp.log(1.0 + jnp.exp(-jnp.abs(x)))


def _silu(x):
    return x / (1.0 + jnp.exp(-x))


def _gelu_tanh(x):
    return 0.5 * x * (1.0 + jnp.tanh(0.7978845608028654 * (x + 0.044715 * (x * x * x))))


def _dot(a, b):
    return jnp.dot(a, b, preferred_element_type=F32)


def _dot_nt(a, b):
    return lax.dot_general(a, b, (((1,), (1,)), ((), ())), preferred_element_type=F32)


def _rowwise(n_rows, fn):
    rc = ROW_CHUNK if n_rows % ROW_CHUNK == 0 else n_rows
    if n_rows == rc:
        fn(0, rc)
        return

    def body(r, carry):
        fn(pl.multiple_of(r * rc, rc), rc)
        return carry

    lax.fori_loop(0, n_rows // rc, body, 0)


def _fill_normed(x_ref, g_ref, xn_ref, row_offset=0):
    g = g_ref[...]

    def step(rs, rc):
        xn_ref[pl.ds(row_offset + rs, rc), :] = _rms(x_ref[pl.ds(rs, rc), :], g).astype(BF16)

    _rowwise(x_ref.shape[0], step)


def _fill_normed_with_halo(x_ref, halo_ref, g_ref, xn_ref, at_seq_start):
    hn = _rms(halo_ref[...], g_ref[...])
    xn_ref[0:HALO, :] = jnp.where(at_seq_start, 0.0, hn).astype(BF16)
    _fill_normed(x_ref, g_ref, xn_ref, row_offset=HALO)


def _causal_conv(h, cw, cb, tm):
    taps = cw.shape[0]
    out = cb
    for k in range(taps):
        off = HALO - (taps - 1) + k
        out = out + cw[k:k + 1, :] * h[off:off + tm, :]
    return out


def _halo_index(tm):
    return lambda i, j: (jnp.maximum(i * (tm // HALO) - 1, 0), 0)


def _norm_matmul_kernel(x_ref, g_ref, w_ref, cs_ref, o_ref, xn_ref, *, act, head_major, sub):
    @pl.when(pl.program_id(1) == 0)
    def _():
        _fill_normed(x_ref, g_ref, xn_ref)

    xn = xn_ref[...]
    tn = w_ref.shape[1]
    for c in range(tn // sub):
        sl = slice(c * sub, (c + 1) * sub)
        y = _dot(xn, w_ref[:, sl]) * cs_ref[:, sl]
        if act == "gelu":
            y = _gelu_tanh(y)
        y = y.astype(o_ref.dtype)
        if head_major:
            for hh in range(sub // LANES):
                o_ref[c * (sub // LANES) + hh] = y[:, hh * LANES:(hh + 1) * LANES]
        else:
            o_ref[:, sl] = y


def _norm_matmul(x, g, w, colscale, *, act=None, head_major=False):
    m, d = x.shape
    n = w.shape[1]
    tm = _pick(m, (1024, 512, 256, 128))
    tn = _pick(n, (1024, 512, 256, 128))
    sub = _pick(tn, (256, 128))
    if head_major:
        out_shape = jax.ShapeDtypeStruct((n // LANES, m, LANES), BF16)
        out_spec = pl.BlockSpec((tn // LANES, tm, LANES), lambda i, j: (j, i, 0))
    else:
        out_shape = jax.ShapeDtypeStruct((m, n), BF16)
        out_spec = pl.BlockSpec((tm, tn), lambda i, j: (i, j))
    return pl.pallas_call(
        functools.partial(_norm_matmul_kernel, act=act, head_major=head_major, sub=sub),
        grid=(m // tm, n // tn),
        in_specs=[
            pl.BlockSpec((tm, d), lambda i, j: (i, 0)),
            pl.BlockSpec((1, d), lambda i, j: (0, 0)),
            pl.BlockSpec((d, tn), lambda i, j: (0, j)),
            pl.BlockSpec((1, tn), lambda i, j: (0, j)),
        ],
        out_specs=out_spec,
        out_shape=out_shape,
        scratch_shapes=[pltpu.VMEM((tm, d), BF16)],
        compiler_params=_params("parallel", "arbitrary"),
        name="norm_matmul",
    )(x, g, w, colscale)


def _finish_residual(acc_ref, res_ref, g_ref, o_ref):
    g = g_ref[...]

    def step(rs, rc):
        rows = pl.ds(rs, rc)
        o_ref[rows, :] = res_ref[rows, :] + _rms(acc_ref[rows, :], g)

    _rowwise(acc_ref.shape[0], step)


def _matmul_norm_res_kernel(a_ref, w_ref, res_ref, g_ref, o_ref, acc_ref):
    k = pl.program_id(1)
    part = _dot(a_ref[...], w_ref[...])

    @pl.when(k == 0)
    def _():
        acc_ref[...] = part

    @pl.when(k > 0)
    def _():
        acc_ref[...] += part

    @pl.when(k == pl.num_programs(1) - 1)
    def _():
        _finish_residual(acc_ref, res_ref, g_ref, o_ref)


def _matmul_norm_res(a, w, res, g):
    m, kdim = a.shape
    d = w.shape[1]
    tm = _pick(m, (1024, 512, 256, 128))
    tk = _pick(kdim, (512, 256, 128))
    return pl.pallas_call(
        _matmul_norm_res_kernel,
        grid=(m // tm, kdim // tk),
        in_specs=[
            pl.BlockSpec((tm, tk), lambda i, k: (i, k)),
            pl.BlockSpec((tk, d), lambda i, k: (k, 0)),
            pl.BlockSpec((tm, d), lambda i, k: (i, 0)),
            pl.BlockSpec((1, d), lambda i, k: (0, 0)),
        ],
        out_specs=pl.BlockSpec((tm, d), lambda i, k: (i, 0)),
        out_shape=jax.ShapeDtypeStruct((m, d), F32),
        scratch_shapes=[pltpu.VMEM((tm, d), F32)],
        compiler_params=_params("parallel", "arbitrary"),
        name="matmul_norm_res",
    )(a, w, res, g)


def _ffn_in_kernel(x_ref, halo_ref, g_ref, wg_ref, wu_ref, cwg_ref, cwu_ref, cbg_ref, cbu_ref,
                   o_ref, xn_ref, *, tiles_per_seq, sub):
    i = pl.program_id(0)

    @pl.when(pl.program_id(1) == 0)
    def _():
        _fill_normed_with_halo(x_ref, halo_ref, g_ref, xn_ref, i % tiles_per_seq == 0)

    xn = xn_ref[...]
    tm, tn = o_ref.shape
    for c in range(tn // sub):
        sl = slice(c * sub, (c + 1) * sub)
        gate = _causal_conv(_dot(xn, wg_ref[:, sl]), cwg_ref[:, sl], cbg_ref[:, sl], tm)
        up = _causal_conv(_dot(xn, wu_ref[:, sl]), cwu_ref[:, sl], cbu_ref[:, sl], tm)
        o_ref[:, sl] = (_gelu_tanh(gate) * up).astype(o_ref.dtype)


def _ffn_in(x, g, w_in, conv_w, conv_b, seq):
    m, d = x.shape
    f = w_in.shape[1] // 2
    taps = conv_w.shape[0]
    tm = _pick(seq, (1024, 512, 256, 128))
    tn = _pick(f, (512, 256, 128))
    sub = _pick(tn, (256, 128))
    nj = f // tn
    return pl.pallas_call(
        functools.partial(_ffn_in_kernel, tiles_per_seq=seq // tm, sub=sub),
        grid=(m // tm, nj),
        in_specs=[
            pl.BlockSpec((tm, d), lambda i, j: (i, 0)),
            pl.BlockSpec((HALO, d), _halo_index(tm)),
            pl.BlockSpec((1, d), lambda i, j: (0, 0)),
            pl.BlockSpec((d, tn), lambda i, j: (0, j)),
            pl.BlockSpec((d, tn), lambda i, j: (0, j + nj)),
            pl.BlockSpec((taps, tn), lambda i, j: (0, j)),
            pl.BlockSpec((taps, tn), lambda i, j: (0, j + nj)),
            pl.BlockSpec((1, tn), lambda i, j: (0, j)),
            pl.BlockSpec((1, tn), lambda i, j: (0, j + nj)),
        ],
        out_specs=pl.BlockSpec((tm, tn), lambda i, j: (i, j)),
        out_shape=jax.ShapeDtypeStruct((m, f), BF16),
        scratch_shapes=[pltpu.VMEM((HALO + tm, d), BF16)],
        compiler_params=_params("parallel", "arbitrary"),
        name="ffn_in",
    )(x, x, g, w_in, w_in, conv_w, conv_w, conv_b, conv_b)


def _mem_kv_kernel(mem_ref, g_ref, w_ref, o_ref):
    o_ref[...] = _dot(_rms(mem_ref[...], g_ref[...]).astype(BF16), w_ref[...]).astype(o_ref.dtype)


def _mem_kv(mem2d, g, wkv, mem_len):
    rows, d = mem2d.shape
    n = wkv.shape[1]
    return pl.pallas_call(
        _mem_kv_kernel,
        grid=(rows // mem_len,),
        in_specs=[
            pl.BlockSpec((mem_len, d), lambda b: (b, 0)),
            pl.BlockSpec((1, d), lambda b: (0, 0)),
            pl.BlockSpec((d, n), lambda b: (0, 0)),
        ],
        out_specs=pl.BlockSpec((mem_len, n), lambda b: (b, 0)),
        out_shape=jax.ShapeDtypeStruct((rows, n), BF16),
        compiler_params=_params("parallel"),
        name="mem_kv",
    )(mem2d, g, wkv)


def _xattn_kernel(x_ref, gpre_ref, wq_ref, kv_ref, wo_ref, gpost_ref, o_ref):
    gpre = gpre_ref[...]
    gpost = gpost_ref[...]
    width = XA_HEADS * XA_HEAD_DIM
    scale = XA_HEAD_DIM ** -0.5

    def step(rs, rc):
        rows = pl.ds(rs, rc)
        x = x_ref[rows, :]
        q = (_dot(_rms(x, gpre).astype(BF16), wq_ref[...]) * scale).astype(BF16)
        heads = []
        for h in range(XA_HEADS):
            lo = h * XA_HEAD_DIM
            kh = kv_ref[:, lo:lo + XA_HEAD_DIM]
            vh = kv_ref[:, width + lo:width + lo + XA_HEAD_DIM]
            s = _dot_nt(q[:, lo:lo + XA_HEAD_DIM], kh)
            p = jnp.exp(s - jnp.max(s, axis=-1, keepdims=True))
            inv = 1.0 / jnp.sum(p, axis=-1, keepdims=True)
            heads.append(_dot(p.astype(BF16), vh) * inv)
        o = jnp.concatenate(heads, axis=1).astype(BF16)
        o_ref[rows, :] = x + _rms(_dot(o, wo_ref[...]), gpost)

    _rowwise(x_ref.shape[0], step)


def _xattn(x, gpre, wq, kv, wo, gpost, seq, mem_len):
    m, d = x.shape
    width = wq.shape[1]
    tm = _pick(seq, (512, 256, 128))
    tps = seq // tm
    return pl.pallas_call(
        _xattn_kernel,
        grid=(m // tm,),
        in_specs=[
            pl.BlockSpec((tm, d), lambda i: (i, 0)),
            pl.BlockSpec((1, d), lambda i: (0, 0)),
            pl.BlockSpec((d, width), lambda i: (0, 0)),
            pl.BlockSpec((mem_len, 2 * width), lambda i: (i // tps, 0)),
            pl.BlockSpec((width, d), lambda i: (0, 0)),
            pl.BlockSpec((1, d), lambda i: (0, 0)),
        ],
        out_specs=pl.BlockSpec((tm, d), lambda i: (i, 0)),
        out_shape=jax.ShapeDtypeStruct((m, d), F32),
        compiler_params=_params("parallel"),
        name="xattn",
    )(x, gpre, wq, kv, wo, gpost)


def _sb_attn_kernel(q_ref, k_ref, v_ref, u_ref, o_ref, acc_ref, c_ref, *, blk):
    i = pl.program_id(2)
    q = q_ref[0]
    row = lax.broadcasted_iota(jnp.int32, (blk, blk), 0)
    col = lax.broadcasted_iota(jnp.int32, (blk, blk), 1)
    valid = col < row

    def block(j, diagonal):
        ks = pl.multiple_of(j * blk, blk)
        z = _dot_nt(q, k_ref[0, pl.ds(ks, blk), :])
        sp = _softplus(z)
        if diagonal:
            sp = jnp.where(valid, sp, 0.0)
        hi = sp.astype(BF16)
        lo = (sp - hi.astype(F32)).astype(BF16)
        sx = _dot(hi, u_ref[...]) + _dot(lo, u_ref[...])
        later = c_ref[...]
        s = sx[:, :blk] + jnp.concatenate([later] * (blk // LANES), axis=1)
        a = jnp.exp(z - s)
        if diagonal:
            a = jnp.where(valid, a, 0.0)
        acc_ref[...] += _dot(a.astype(BF16), v_ref[0, pl.ds(ks, blk), :])
        c_ref[...] = later + sx[:, blk:]

    acc_ref[...] = jnp.zeros_like(acc_ref)
    c_ref[...] = jnp.zeros_like(c_ref)
    block(i, True)

    def body(jj, carry):
        block(i - jj, False)
        return carry

    lax.fori_loop(1, i + 1, body, 0)
    o_ref[...] = acc_ref[...].astype(o_ref.dtype)


def _sb_attn(qkv, batch, seq):
    heads = qkv.shape[0] // 3
    m = batch * seq
    blk = _pick(seq, (256, 128))
    nq = seq // blk
    key = lax.broadcasted_iota(jnp.int32, (blk, blk + LANES), 0)
    out = lax.broadcasted_iota(jnp.int32, (blk, blk + LANES), 1)
    u = ((key >= out) | (out >= blk)).astype(BF16)
    return pl.pallas_call(
        functools.partial(_sb_attn_kernel, blk=blk),
        grid=(batch, heads, nq),
        in_specs=[
            pl.BlockSpec((1, blk, SB_HEAD_DIM), lambda b, h, i: (h, b * nq + i, 0)),
            pl.BlockSpec((1, seq, SB_HEAD_DIM), lambda b, h, i: (heads + h, b, 0)),
            pl.BlockSpec((1, seq, SB_HEAD_DIM), lambda b, h, i: (2 * heads + h, b, 0)),
            pl.BlockSpec((blk, blk + LANES), lambda b, h, i: (0, 0)),
        ],
        out_specs=pl.BlockSpec((blk, SB_HEAD_DIM), lambda b, h, i: (b * nq + i, h)),
        out_shape=jax.ShapeDtypeStruct((m, heads * SB_HEAD_DIM), BF16),
        scratch_shapes=[pltpu.VMEM((blk, SB_HEAD_DIM), F32), pltpu.VMEM((blk, LANES), F32)],
        compiler_params=_params("parallel", "parallel", "arbitrary"),
        name="sb_attn",
    )(qkv, qkv, qkv, u)


def _sgu_out_kernel(u_ref, v_ref, vfull_ref, lng_ref, lnb_ref, ws_ref, bs_ref, w_ref, res_ref, g_ref,
                    o_ref, acc_ref, mu_ref, rstd_ref, p_ref):
    k = pl.program_id(1)
    tm, tk = u_ref.shape
    gw = tk // ws_ref.shape[0]

    @pl.when(k == 0)
    def _():
        def stats(rs, rc):
            rows = pl.ds(rs, rc)
            v = vfull_ref[rows, :].astype(F32)
            mu = jnp.mean(v, axis=-1, keepdims=True)
            vc = v - mu
            mu_ref[rows, :] = mu
            rstd_ref[rows, :] = lax.rsqrt(jnp.mean(vc * vc, axis=-1, keepdims=True) + EPS)

        _rowwise(tm, stats)

    row = lax.broadcasted_iota(jnp.int32, (SG_CHUNK, SG_CHUNK), 0)
    col = lax.broadcasted_iota(jnp.int32, (SG_CHUNK, SG_CHUNK), 1)
    for gi in range(ws_ref.shape[0]):
        cols = slice(gi * gw, (gi + 1) * gw)
        ws = jnp.where(col <= row, ws_ref[gi], 0.0).astype(BF16)
        bias = jnp.concatenate([bs_ref[gi]] * (gw // LANES), axis=1)
        for c in range(tm // SG_CHUNK):
            rows = slice(c * SG_CHUNK, (c + 1) * SG_CHUNK)
            vn = (v_ref[rows, cols].astype(F32) - mu_ref[rows, :]) * rstd_ref[rows, :]
            vn = (vn * lng_ref[:, cols] + lnb_ref[:, cols]).astype(BF16)
            mixed = _dot(ws, vn) + bias
            p_ref[rows, cols] = (u_ref[rows, cols].astype(F32) * mixed).astype(BF16)

    part = _dot(p_ref[...], w_ref[...])

    @pl.when(k == 0)
    def _():
        acc_ref[...] = part

    @pl.when(k > 0)
    def _():
        acc_ref[...] += part

    @pl.when(k == pl.num_programs(1) - 1)
    def _():
        _finish_residual(acc_ref, res_ref, g_ref, o_ref)


def _sgu_out(uv, ln_g, ln_b, w_spatial, b_spatial_lanes, w_out, res, g, seq):
    m = uv.shape[0]
    width = uv.shape[1] // 2
    d = w_out.shape[1]
    gw = width // SG_GROUPS
    tm = _pick(seq, (512, 256, 128))
    tk = _pick(width, (512, 256, 128))
    tk = max(tk, gw)
    gps = tk // gw
    nk = width // tk
    return pl.pallas_call(
        _sgu_out_kernel,
        grid=(m // tm, nk),
        in_specs=[
            pl.BlockSpec((tm, tk), lambda i, k: (i, k)),
            pl.BlockSpec((tm, tk), lambda i, k: (i, k + nk)),
            pl.BlockSpec((tm, width), lambda i, k: (i, 1)),
            pl.BlockSpec((1, tk), lambda i, k: (0, k)),
            pl.BlockSpec((1, tk), lambda i, k: (0, k)),
            pl.BlockSpec((gps, SG_CHUNK, SG_CHUNK), lambda i, k: (k, 0, 0)),
            pl.BlockSpec((gps, SG_CHUNK, LANES), lambda i, k: (k, 0, 0)),
            pl.BlockSpec((tk, d), lambda i, k: (k, 0)),
            pl.BlockSpec((tm, d), lambda i, k: (i, 0)),
            pl.BlockSpec((1, d), lambda i, k: (0, 0)),
        ],
        out_specs=pl.BlockSpec((tm, d), lambda i, k: (i, 0)),
        out_shape=jax.ShapeDtypeStruct((m, d), F32),
        scratch_shapes=[
            pltpu.VMEM((tm, d), F32),
            pltpu.VMEM((tm, 1), F32),
            pltpu.VMEM((tm, 1), F32),
            pltpu.VMEM((tm, tk), BF16),
        ],
        compiler_params=_params("parallel", "arbitrary"),
        name="sgu_out",
    )(uv, uv, uv, ln_g, ln_b, w_spatial, b_spatial_lanes, w_out, res, g)


def _ssd_in_kernel(x_ref, halo_ref, g_ref, w_ref, wdt_ref, dtb_ref, cw_ref, cb_ref,
                   o_ref, dt_ref, xn_ref, *, tiles_per_seq, nz, sub):
    i = pl.program_id(0)
    j = pl.program_id(1)
    tm, tn = o_ref.shape

    @pl.when(j == 0)
    def _():
        _fill_normed_with_halo(x_ref, halo_ref, g_ref, xn_ref, i % tiles_per_seq == 0)
        dt_ref[...] = _softplus(_dot(xn_ref[HALO:, :], wdt_ref[...]) + dtb_ref[...])

    @pl.when(j < nz)
    def _():
        for c in range(tn // sub):
            sl = slice(c * sub, (c + 1) * sub)
            o_ref[:, sl] = _dot(xn_ref[HALO:, :], w_ref[:, sl]).astype(o_ref.dtype)

    @pl.when(j >= nz)
    def _():
        xn = xn_ref[...]
        for c in range(tn // sub):
            sl = slice(c * sub, (c + 1) * sub)
            y = _causal_conv(_dot(xn, w_ref[:, sl]), cw_ref[:, sl], cb_ref[:, sl], tm)
            o_ref[:, sl] = _silu(y).astype(o_ref.dtype)


def _ssd_in(x, g, w_main, w_dt, dt_bias, conv_w, conv_b, d_inner, seq):
    m, d = x.shape
    n = w_main.shape[1]
    taps = conv_w.shape[0]
    tm = _pick(seq, (1024, 512, 256, 128))
    tn = _pick(d_inner, (512, 256, 128))
    tn = _pick(n - d_inner, (tn,))
    sub = _pick(tn, (256, 128))
    nz = d_inner // tn
    return pl.pallas_call(
        functools.partial(_ssd_in_kernel, tiles_per_seq=seq // tm, nz=nz, sub=sub),
        grid=(m // tm, n // tn),
        in_specs=[
            pl.BlockSpec((tm, d), lambda i, j: (i, 0)),
            pl.BlockSpec((HALO, d), _halo_index(tm)),
            pl.BlockSpec((1, d), lambda i, j: (0, 0)),
            pl.BlockSpec((d, tn), lambda i, j: (0, j)),
            pl.BlockSpec((d, LANES), lambda i, j: (0, 0)),
            pl.BlockSpec((1, LANES), lambda i, j: (0, 0)),
            pl.BlockSpec((taps, tn), lambda i, j: (0, jnp.maximum(j - nz, 0))),
            pl.BlockSpec((1, tn), lambda i, j: (0, jnp.maximum(j - nz, 0))),
        ],
        out_specs=[
            pl.BlockSpec((tm, tn), lambda i, j: (i, j)),
            pl.BlockSpec((tm, LANES), lambda i, j: (i, 0)),
        ],
        out_shape=[
            jax.ShapeDtypeStruct((m, n), BF16),
            jax.ShapeDtypeStruct((m, LANES), F32),
        ],
        scratch_shapes=[pltpu.VMEM((HALO + tm, d), BF16)],
        compiler_params=_params("parallel", "arbitrary"),
        name="ssd_in",
    )(x, x, g, w_main, w_dt, dt_bias, conv_w, conv_b)


def _exact_tril_matmul(tril, a):
    a1 = a.astype(BF16)
    r1 = a - a1.astype(F32)
    a2 = r1.astype(BF16)
    a3 = (r1 - a2.astype(F32)).astype(BF16)
    return _dot(tril, a1) + _dot(tril, a2) + _dot(tril, a3)


def _ssd_scan_kernel(z_ref, xs_ref, bc_ref, dt_ref, a_ref, dskip_ref, ng_ref, tril_ref,
                     o_ref, state_ref, y_ref, *, heads_per_group):
    q = SSD_CHUNK
    n = SSD_STATE
    p = SSD_HEAD_DIM
    groups = SSD_GROUPS
    pairs = heads_per_group // 2
    gwidth = heads_per_group * p

    @pl.when(pl.program_id(1) == 0)
    def _():
        state_ref[...] = jnp.zeros_like(state_ref)

    dt = dt_ref[...]
    a_cum = _exact_tril_matmul(tril_ref[...], dt * a_ref[...])
    a_cum_t = a_cum.T
    dt_t = dt.T
    a_last = jnp.broadcast_to(a_cum_t[:, q - 1:q], (LANES, q))
    w_t = jnp.exp(a_last - a_cum_t) * dt_t
    chunk_decay = jnp.exp(a_last)

    row = lax.broadcasted_iota(jnp.int32, (q, q), 0)
    col = lax.broadcasted_iota(jnp.int32, (q, q), 1)
    causal = col <= row
    low_half = lax.broadcasted_iota(jnp.int32, (q, LANES), 1) < p

    for g in range(groups):
        bg = bc_ref[:, g * n:(g + 1) * n]
        cg = bc_ref[:, (groups + g) * n:(groups + g + 1) * n]
        cb = _dot_nt(cg, bg)
        bg_t = bg.astype(F32).T
        y_off = _dot(cg, state_ref[g].astype(BF16))
        for pr in range(pairs):
            h0 = g * heads_per_group + 2 * pr
            lanes = slice(g * gwidth + pr * LANES, g * gwidth + (pr + 1) * LANES)
            glanes = slice(pr * LANES, (pr + 1) * LANES)
            top, bottom, decay_in = [], [], []
            for h in (h0, h0 + 1):
                along_rows = jnp.broadcast_to(a_cum[:, h:h + 1], (q, q))
                along_cols = jnp.broadcast_to(a_cum_t[h:h + 1, :], (q, q))
                decay = jnp.where(causal, jnp.exp(along_rows - along_cols), 0.0)
                top.append((cb * decay * dt_t[h:h + 1, :]).astype(BF16))
                bottom.append((bg_t * w_t[h:h + 1, :]).astype(BF16))
                decay_in.append(along_rows)
            lhs = jnp.concatenate([jnp.concatenate(top, axis=1), jnp.concatenate(bottom, axis=1)], axis=0)
            xp = xs_ref[:, lanes]
            zero = jnp.zeros_like(xp)
            rhs = jnp.concatenate([jnp.where(low_half, xp, zero), jnp.where(low_half, zero, xp)], axis=0)
            res = _dot(lhs, rhs)
            enter = jnp.exp(jnp.where(low_half, decay_in[0], decay_in[1]))
            y_ref[:, lanes] = res[:q] + y_off[:, glanes] * enter + xp.astype(F32) * dskip_ref[:, lanes]
            carry = jnp.where(low_half[:n], chunk_decay[h0:h0 + 1, :], chunk_decay[h0 + 1:h0 + 2, :])
            state_ref[g, :, glanes] = state_ref[g, :, glanes] * carry + res[q:]

    y = y_ref[...] * _silu(z_ref[...].astype(F32))
    o_ref[...] = _rms(y, ng_ref[...]).astype(o_ref.dtype)


def _ssd_scan(proj, dt, a_row, dskip, norm_g, batch, seq, d_inner):
    m = proj.shape[0]
    q = SSD_CHUNK
    nc = seq // q
    heads = d_inner // SSD_HEAD_DIM
    hpg = heads // SSD_GROUPS
    bc_width = 2 * SSD_GROUPS * SSD_STATE
    assert SSD_HEAD_DIM * 2 == LANES and hpg % 2 == 0 and d_inner % bc_width == 0
    tril = (lax.broadcasted_iota(jnp.int32, (q, q), 1) <= lax.broadcasted_iota(jnp.int32, (q, q), 0)).astype(BF16)
    rows = lambda b, c: b * nc + c
    return pl.pallas_call(
        functools.partial(_ssd_scan_kernel, heads_per_group=hpg),
        grid=(batch, nc),
        in_specs=[
            pl.BlockSpec((q, d_inner), lambda b, c: (rows(b, c), 0)),
            pl.BlockSpec((q, d_inner), lambda b, c: (rows(b, c), 1)),
            pl.BlockSpec((q, bc_width), lambda b, c: (rows(b, c), 2 * d_inner // bc_width)),
            pl.BlockSpec((q, LANES), lambda b, c: (rows(b, c), 0)),
            pl.BlockSpec((1, LANES), lambda b, c: (0, 0)),
            pl.BlockSpec((1, d_inner), lambda b, c: (0, 0)),
            pl.BlockSpec((1, d_inner), lambda b, c: (0, 0)),
            pl.BlockSpec((q, q), lambda b, c: (0, 0)),
        ],
        out_specs=pl.BlockSpec((q, d_inner), lambda b, c: (rows(b, c), 0)),
        out_shape=jax.ShapeDtypeStruct((m, d_inner), BF16),
        scratch_shapes=[
            pltpu.VMEM((SSD_GROUPS, SSD_STATE, hpg * SSD_HEAD_DIM), F32),
            pltpu.VMEM((q, d_inner), F32),
        ],
        compiler_params=_params("parallel", "arbitrary"),
        name="ssd_scan",
    )(proj, proj, proj, dt, a_row, dskip, norm_g, tril)


def _row(v):
    return v.reshape(1, -1).astype(F32)


def _pad_lanes(v):
    return jnp.pad(v, [(0, 0)] * (v.ndim - 1) + [(0, LANES - v.shape[-1])])


def _ssd_mixer(x, g_pre, w_in, conv_w, conv_b, dt_bias, a_log, d_skip, norm_g, w_out, g_post, batch, seq):
    d_inner = w_out.shape[0]
    heads = d_inner // SSD_HEAD_DIM
    n_main = w_in.shape[1] - heads
    w_main = w_in[:, :n_main].astype(BF16)
    w_dt = _pad_lanes(w_in[:, n_main:]).astype(BF16)
    proj, dt = _ssd_in(x, g_pre, w_main, w_dt, _pad_lanes(_row(dt_bias)), conv_w, _row(conv_b), d_inner, seq)
    a_row = _pad_lanes(_row(-jnp.exp(a_log.astype(F32))))
    dskip = _row(jnp.repeat(d_skip.astype(F32), SSD_HEAD_DIM))
    y = _ssd_scan(proj, dt, a_row, dskip, _row(norm_g), batch, seq, d_inner)
    return _matmul_norm_res(y, w_out.astype(BF16), x, g_post)


def _sgu_mixer(x, g_pre, w_in, ln_g, ln_b, w_spatial, b_spatial, w_out, g_post, seq):
    n = w_in.shape[1]
    uv = _norm_matmul(x, g_pre, w_in.astype(BF16), jnp.ones((1, n), F32), act="gelu")
    b_lanes = jnp.broadcast_to(b_spatial.astype(F32)[:, :, None], b_spatial.shape + (LANES,))
    return _sgu_out(uv, _row(ln_g), _row(ln_b), w_spatial.astype(F32), b_lanes, w_out.astype(BF16), x, g_post, seq)


def _sb_mixer(x, g_pre, w_qkv, w_out, g_post, batch, seq):
    d = x.shape[1]
    colscale = jnp.concatenate([jnp.full((1, d), SB_HEAD_DIM ** -0.5, F32), jnp.ones((1, 2 * d), F32)], axis=1)
    qkv = _norm_matmul(x, g_pre, w_qkv.astype(BF16), colscale, head_major=True)
    o = _sb_attn(qkv, batch, seq)
    return _matmul_norm_res(o, w_out.astype(BF16), x, g_post)


def kernel(x, mem, ln_mix_pre, ln_mix_post, ln_mem, ln_xa_pre, ln_xa_post, ln_ffn_pre, ln_ffn_post, xa_wq, xa_wkv, xa_wo, ffn_w_in, ffn_conv_w, ffn_conv_b, ffn_w_out, ssd_w_in, ssd_conv_w, ssd_conv_b, ssd_dt_bias, ssd_a_log, ssd_d, ssd_norm, ssd_w_out, sg_w_in, sg_v_norm_g, sg_v_norm_b, sg_w_spatial, sg_b_spatial, sg_w_out, sb_w_qkv, sb_w_out):
    batch, seq, d = x.shape
    mem_len = mem.shape[1]
    depth = ln_mix_pre.shape[0]
    h = x.reshape(batch * seq, d).astype(F32)
    mem2d = mem.reshape(batch * mem_len, d).astype(F32)
    for i in range(depth):
        kind, j = i % 3, i // 3
        g_pre, g_post = _row(ln_mix_pre[i]), _row(ln_mix_post[i])
        if kind == 0:
            h = _ssd_mixer(h, g_pre, ssd_w_in[j], ssd_conv_w[j].astype(F32), ssd_conv_b[j], ssd_dt_bias[j],
                           ssd_a_log[j], ssd_d[j], ssd_norm[j], ssd_w_out[j], g_post, batch, seq)
        elif kind == 1:
            h = _sgu_mixer(h, g_pre, sg_w_in[j], sg_v_norm_g[j], sg_v_norm_b[j], sg_w_spatial[j],
                           sg_b_spatial[j], sg_w_out[j], g_post, seq)
        else:
            h = _sb_mixer(h, g_pre, sb_w_qkv[j], sb_w_out[j], g_post, batch, seq)
        kv = _mem_kv(mem2d, _row(ln_mem[i]), xa_wkv[i].astype(BF16), mem_len)
        h = _xattn(h, _row(ln_xa_pre[i]), xa_wq[i].astype(BF16), kv, xa_wo[i].astype(BF16),
                   _row(ln_xa_post[i]), seq, mem_len)
        f = _ffn_in(h, _row(ln_ffn_pre[i]), ffn_w_in[i].astype(BF16), ffn_conv_w[i].astype(F32),
                    _row(ffn_conv_b[i]), seq)
        h = _matmul_norm_res(f, ffn_w_out[i].astype(BF16), h, _row(ln_ffn_post[i]))
    return h.reshape(batch, seq, d).astype(x.dtype)
```

```python
import functools

import jax
import jax.numpy as jnp
from jax import lax
from jax.experimental import pallas as pl
from jax.experimental.pallas import tpu as pltpu

F32 = jnp.float32
BF16 = jnp.bfloat16

EPS = 1e-6
LANES = 128
HALO = 16
VMEM_LIMIT_BYTES = 56 * 1024 * 1024

SSD_HEAD_DIM = 64
SSD_GROUPS = 8
SSD_STATE = 128
SSD_CHUNK = 128
SG_GROUPS = 16
SG_CHUNK = 128
SB_HEAD_DIM = 128
SB_ZERO_WEIGHT_SUM = 105.0
XA_HEADS = 4
XA_HEAD_DIM = 128

ROW_CHUNK = 256


def _params(*sem):
    return pltpu.CompilerParams(dimension_semantics=sem, vmem_limit_bytes=VMEM_LIMIT_BYTES)


def _pick(n, prefs):
    for p in prefs:
        if n % p == 0:
            return p
    raise ValueError(f"no tile in {prefs} divides {n}")


def _rms(x, g):
    ms = jnp.mean(x * x, axis=-1, keepdims=True)
    return x * lax.rsqrt(ms + EPS) * g


def _softplus(x):
    return jnp.maximum(x, 0.0) + jnp.log(1.0 + jnp.exp(-jnp.abs(x)))


def _silu(x):
    return x / (1.0 + jnp.exp(-x))


_GELU_A = -2.0 * 0.7978845608028654
_GELU_B = _GELU_A * 0.044715


def _gelu_tanh_times(x, y):
    return x * y / (1.0 + jnp.exp(x * (_GELU_B * (x * x) + _GELU_A)))


def _dot(a, b):
    return jnp.dot(a, b, preferred_element_type=F32)


def _dot_nt(a, b):
    return lax.dot_general(a, b, (((1,), (1,)), ((), ())), preferred_element_type=F32)


def _rowwise(n_rows, fn):
    rc = ROW_CHUNK if n_rows % ROW_CHUNK == 0 else n_rows
    if n_rows == rc:
        fn(0, rc)
        return

    def body(r, carry):
        fn(pl.multiple_of(r * rc, rc), rc)
        return carry

    lax.fori_loop(0, n_rows // rc, body, 0)


def _fill_normed(x_ref, g_ref, xn_ref, row_offset=0):
    g = g_ref[...]

    def step(rs, rc):
        xn_ref[pl.ds(row_offset + rs, rc), :] = _rms(x_ref[pl.ds(rs, rc), :], g).astype(BF16)

    _rowwise(x_ref.shape[0], step)


def _fill_normed_with_halo(x_ref, halo_ref, g_ref, xn_ref, at_seq_start):
    hn = _rms(halo_ref[...], g_ref[...])
    xn_ref[0:HALO, :] = jnp.where(at_seq_start, 0.0, hn).astype(BF16)
    _fill_normed(x_ref, g_ref, xn_ref, row_offset=HALO)


def _halo_index(tm):
    return lambda i, j: (jnp.maximum(i * (tm // HALO) - 1, 0), 0)


def _norm_matmul_kernel(x_ref, g_ref, w_ref, cs_ref, o_ref, xn_ref, *, act, head_major, sub):
    @pl.when(pl.program_id(1) == 0)
    def _():
        _fill_normed(x_ref, g_ref, xn_ref)

    xn = xn_ref[...]
    tn = w_ref.shape[1]
    for c in range(tn // sub):
        sl = slice(c * sub, (c + 1) * sub)
        y = _dot(xn, w_ref[:, sl]) * cs_ref[:, sl]
        if act == "gelu":
            y = _gelu_tanh_times(y, 1.0)
        y = y.astype(o_ref.dtype)
        if head_major:
            for hh in range(sub // LANES):
                o_ref[c * (sub // LANES) + hh] = y[:, hh * LANES:(hh + 1) * LANES]
        else:
            o_ref[:, sl] = y


def _norm_matmul(x, g, w, colscale, *, act=None, head_major=False):
    m, d = x.shape
    n = w.shape[1]
    tm = _pick(m, (1024, 512, 256, 128))
    tn = _pick(n, (1024, 512, 256, 128))
    sub = _pick(tn, (256, 128))
    if head_major:
        out_shape = jax.ShapeDtypeStruct((n // LANES, m, LANES), BF16)
        out_spec = pl.BlockSpec((tn // LANES, tm, LANES), lambda i, j: (j, i, 0))
    else:
        out_shape = jax.ShapeDtypeStruct((m, n), BF16)
        out_spec = pl.BlockSpec((tm, tn), lambda i, j: (i, j))
    return pl.pallas_call(
        functools.partial(_norm_matmul_kernel, act=act, head_major=head_major, sub=sub),
        grid=(m // tm, n // tn),
        in_specs=[
            pl.BlockSpec((tm, d), lambda i, j: (i, 0)),
            pl.BlockSpec((1, d), lambda i, j: (0, 0)),
            pl.BlockSpec((d, tn), lambda i, j: (0, j)),
            pl.BlockSpec((1, tn), lambda i, j: (0, j)),
        ],
        out_specs=out_spec,
        out_shape=out_shape,
        scratch_shapes=[pltpu.VMEM((tm, d), BF16)],
        compiler_params=_params("parallel", "arbitrary"),
        name="norm_matmul",
    )(x, g, w, colscale)


def _finish_residual(acc_ref, res_ref, g_ref, o_ref):
    g = g_ref[...]

    def step(rs, rc):
        rows = pl.ds(rs, rc)
        o_ref[rows, :] = res_ref[rows, :] + _rms(acc_ref[rows, :], g)

    _rowwise(acc_ref.shape[0], step)


def _matmul_norm_res_kernel(a_ref, w_ref, res_ref, g_ref, o_ref, acc_ref):
    k = pl.program_id(1)

    @pl.when(k == 0)
    def _():
        acc_ref[...] = _dot(a_ref[...], w_ref[...])

    @pl.when(k > 0)
    def _():
        acc_ref[...] += _dot(a_ref[...], w_ref[...])

    @pl.when(k == pl.num_programs(1) - 1)
    def _():
        _finish_residual(acc_ref, res_ref, g_ref, o_ref)


def _matmul_norm_res(a, w, res, g):
    m, kdim = a.shape
    d = w.shape[1]
    tm = _pick(m, (1024, 512, 256, 128))
    tk = _pick(kdim, (512, 256, 128))
    return pl.pallas_call(
        _matmul_norm_res_kernel,
        grid=(m // tm, kdim // tk),
        in_specs=[
            pl.BlockSpec((tm, tk), lambda i, k: (i, k)),
            pl.BlockSpec((tk, d), lambda i, k: (k, 0)),
            pl.BlockSpec((tm, d), lambda i, k: (i, 0)),
            pl.BlockSpec((1, d), lambda i, k: (0, 0)),
        ],
        out_specs=pl.BlockSpec((tm, d), lambda i, k: (i, 0)),
        out_shape=jax.ShapeDtypeStruct((m, d), F32),
        scratch_shapes=[pltpu.VMEM((tm, d), F32)],
        compiler_params=_params("parallel", "arbitrary"),
        name="matmul_norm_res",
    )(a, w, res, g)


def _conv_proj_kernel(*refs, tiles_per_seq, nj, streams, with_dt, sub):
    it = iter(refs)
    x_ref, halo_ref, g_ref = next(it), next(it), next(it)
    w_refs = [next(it) for _ in range(streams)]
    cw_refs = [next(it) for _ in range(streams)]
    cb_refs = [next(it) for _ in range(streams)]
    if with_dt:
        wdt_ref, dtb_ref = next(it), next(it)
    o_ref = next(it)
    if with_dt:
        dt_ref = next(it)
    xn_ref = next(it)
    raw = [[next(it), next(it)] for _ in range(streams)]

    i = pl.program_id(0)
    j = pl.program_id(1)
    tm, tn = o_ref.shape
    taps = cw_refs[0].shape[0]
    per_sub = sub // LANES
    rc = min(ROW_CHUNK, tm)

    def produce(slot):
        xn = xn_ref[...]
        for s in range(streams):
            for c in range(tn // sub):
                h = _dot(xn, w_refs[s][:, c * sub:(c + 1) * sub])
                for t in range(per_sub):
                    raw[s][slot][c * per_sub + t] = h[:, t * LANES:(t + 1) * LANES]

    def consume(slot):
        for ct in range(tn // LANES):
            lanes = slice(ct * LANES, (ct + 1) * LANES)
            for r0 in range(0, tm, rc):
                vals = []
                for s in range(streams):
                    acc = cb_refs[s][:, lanes]
                    for k in range(taps):
                        off = HALO - (taps - 1) + k + r0
                        acc = acc + cw_refs[s][k:k + 1, lanes] * raw[s][slot][ct, off:off + rc, :]
                    vals.append(acc)
                y = _gelu_tanh_times(vals[0], vals[1]) if streams == 2 else _silu(vals[0])
                o_ref[r0:r0 + rc, lanes] = y.astype(o_ref.dtype)

    @pl.when(j == 0)
    def _():
        _fill_normed_with_halo(x_ref, halo_ref, g_ref, xn_ref, i % tiles_per_seq == 0)
        if with_dt:
            dt_ref[...] = _softplus(_dot(xn_ref[HALO:, :], wdt_ref[...]) + dtb_ref[...])
        produce(0)

    for parity in (0, 1):
        @pl.when((j > 0) & (j < nj) & (j % 2 == parity))
        def _():
            consume(1 - parity)
            produce(parity)

    @pl.when(j == nj)
    def _():
        consume((nj - 1) % 2)


def _conv_proj(x, g, w, conv_w, conv_b, seq, *, streams, dt_proj=None, name):
    m, d = x.shape
    n = w.shape[1] // streams
    taps = conv_w.shape[0]
    tm = _pick(seq, (1024, 512, 256, 128))
    tn = _pick(n, (512, 256, 128))
    sub = _pick(tn, (256, 128))
    nj = n // tn
    with_dt = dt_proj is not None
    produced = lambda s: (lambda i, j: (0, jnp.minimum(j, nj - 1) + s * nj))
    consumed = lambda s: (lambda i, j: (0, jnp.maximum(j - 1, 0) + s * nj))
    in_specs = [
        pl.BlockSpec((tm, d), lambda i, j: (i, 0)),
        pl.BlockSpec((HALO, d), _halo_index(tm)),
        pl.BlockSpec((1, d), lambda i, j: (0, 0)),
    ]
    in_specs += [pl.BlockSpec((d, tn), produced(s)) for s in range(streams)]
    in_specs += [pl.BlockSpec((taps, tn), consumed(s)) for s in range(streams)]
    in_specs += [pl.BlockSpec((1, tn), consumed(s)) for s in range(streams)]
    args = [x, x, g] + [w] * streams + [conv_w] * streams + [conv_b] * streams
    out_specs = [pl.BlockSpec((tm, tn), lambda i, j: (i, jnp.maximum(j - 1, 0)))]
    out_shape = [jax.ShapeDtypeStruct((m, n), BF16)]
    if with_dt:
        in_specs += [pl.BlockSpec((d, LANES), lambda i, j: (0, 0)), pl.BlockSpec((1, LANES), lambda i, j: (0, 0))]
        args += list(dt_proj)
        out_specs.append(pl.BlockSpec((tm, LANES), lambda i, j: (i, 0)))
        out_shape.append(jax.ShapeDtypeStruct((m, LANES), F32))
    raw = pltpu.VMEM((tn // LANES, HALO + tm, LANES), F32)
    outs = pl.pallas_call(
        functools.partial(_conv_proj_kernel, tiles_per_seq=seq // tm, nj=nj, streams=streams,
                          with_dt=with_dt, sub=sub),
        grid=(m // tm, nj + 1),
        in_specs=in_specs,
        out_specs=out_specs,
        out_shape=out_shape,
        scratch_shapes=[pltpu.VMEM((HALO + tm, d), BF16)] + [raw] * (2 * streams),
        compiler_params=_params("parallel", "arbitrary"),
        name=name,
    )(*args)
    return outs if with_dt else outs[0]


def _mem_kv_kernel(mem_ref, g_ref, w_ref, o_ref):
    o_ref[...] = _dot(_rms(mem_ref[...], g_ref[...]).astype(BF16), w_ref[...]).astype(o_ref.dtype)


def _mem_kv(mem2d, g, wkv, mem_len):
    rows, d = mem2d.shape
    n = wkv.shape[1]
    return pl.pallas_call(
        _mem_kv_kernel,
        grid=(rows // mem_len,),
        in_specs=[
            pl.BlockSpec((mem_len, d), lambda b: (b, 0)),
            pl.BlockSpec((1, d), lambda b: (0, 0)),
            pl.BlockSpec((d, n), lambda b: (0, 0)),
        ],
        out_specs=pl.BlockSpec((mem_len, n), lambda b: (b, 0)),
        out_shape=jax.ShapeDtypeStruct((rows, n), BF16),
        compiler_params=_params("parallel"),
        name="mem_kv",
    )(mem2d, g, wkv)


def _xattn_kernel(x_ref, gpre_ref, wq_ref, kv_ref, wo_ref, gpost_ref, o_ref):
    gpre = gpre_ref[...]
    gpost = gpost_ref[...]
    width = XA_HEADS * XA_HEAD_DIM
    scale = XA_HEAD_DIM ** -0.5

    def step(rs, rc):
        rows = pl.ds(rs, rc)
        x = x_ref[rows, :]
        q = (_dot(_rms(x, gpre).astype(BF16), wq_ref[...]) * scale).astype(BF16)
        heads = []
        for h in range(XA_HEADS):
            lo = h * XA_HEAD_DIM
            kh = kv_ref[:, lo:lo + XA_HEAD_DIM]
            vh = kv_ref[:, width + lo:width + lo + XA_HEAD_DIM]
            s = _dot_nt(q[:, lo:lo + XA_HEAD_DIM], kh)
            p = jnp.exp(s - jnp.max(s, axis=-1, keepdims=True))
            inv = 1.0 / jnp.sum(p, axis=-1, keepdims=True)
            heads.append(_dot(p.astype(BF16), vh) * inv)
        o = jnp.concatenate(heads, axis=1).astype(BF16)
        o_ref[rows, :] = x + _rms(_dot(o, wo_ref[...]), gpost)

    _rowwise(x_ref.shape[0], step)


def _xattn(x, gpre, wq, kv, wo, gpost, seq, mem_len):
    m, d = x.shape
    width = wq.shape[1]
    tm = _pick(seq, (512, 256, 128))
    tps = seq // tm
    return pl.pallas_call(
        _xattn_kernel,
        grid=(m // tm,),
        in_specs=[
            pl.BlockSpec((tm, d), lambda i: (i, 0)),
            pl.BlockSpec((1, d), lambda i: (0, 0)),
            pl.BlockSpec((d, width), lambda i: (0, 0)),
            pl.BlockSpec((mem_len, 2 * width), lambda i: (i // tps, 0)),
            pl.BlockSpec((width, d), lambda i: (0, 0)),
            pl.BlockSpec((1, d), lambda i: (0, 0)),
        ],
        out_specs=pl.BlockSpec((tm, d), lambda i: (i, 0)),
        out_shape=jax.ShapeDtypeStruct((m, d), F32),
        compiler_params=_params("parallel"),
        name="xattn",
    )(x, gpre, wq, kv, wo, gpost)


def _sb_attn_kernel(q_ref, k_ref, v_ref, u_ref, o_ref, acc_ref, c_ref, *, blk, hb):
    i = pl.program_id(2)
    row = lax.broadcasted_iota(jnp.int32, (blk, blk), 0)
    col = lax.broadcasted_iota(jnp.int32, (blk, blk), 1)
    valid = col < row

    def block(j, diagonal):
        ks = pl.multiple_of(j * blk, blk)
        smallest = None
        for hh in range(hb):
            z = _dot_nt(q_ref[hh], k_ref[hh, pl.ds(ks, blk), :])
            sp = _softplus(z)
            if diagonal:
                sp = jnp.where(valid, sp, 0.0)
            hi = sp.astype(BF16)
            lo = (sp - hi.astype(F32)).astype(BF16)
            sx = _dot(hi, u_ref[...]) + _dot(lo, u_ref[...])
            later = c_ref[hh]
            s = sx[:, :blk] + jnp.concatenate([later] * (blk // LANES), axis=1)
            a = jnp.exp(z - s)
            if diagonal:
                a = jnp.where(valid, a, 0.0)
            acc_ref[hh] += _dot(a.astype(BF16), v_ref[hh, pl.ds(ks, blk), :])
            total = later + sx[:, blk:]
            c_ref[hh] = total
            smallest = total if smallest is None else jnp.minimum(smallest, total)
        return jnp.min(smallest)

    acc_ref[...] = jnp.zeros_like(acc_ref)
    c_ref[...] = jnp.zeros_like(c_ref)
    first = block(i, True)

    def more(carry):
        jj, smallest = carry
        return (jj <= i) & (smallest < SB_ZERO_WEIGHT_SUM)

    def body(carry):
        jj, _ = carry
        return jj + 1, block(i - jj, False)

    lax.while_loop(more, body, (jnp.int32(1), first))
    for hh in range(hb):
        o_ref[:, hh * SB_HEAD_DIM:(hh + 1) * SB_HEAD_DIM] = acc_ref[hh].astype(o_ref.dtype)


def _sb_attn(qkv, batch, seq):
    heads = qkv.shape[0] // 3
    m = batch * seq
    blk = _pick(seq, (256, 128))
    hb = _pick(heads, (4, 2, 1))
    nq = seq // blk
    nh = heads // hb
    key = lax.broadcasted_iota(jnp.int32, (blk, blk + LANES), 0)
    out = lax.broadcasted_iota(jnp.int32, (blk, blk + LANES), 1)
    u = ((key >= out) | (out >= blk)).astype(BF16)
    return pl.pallas_call(
        functools.partial(_sb_attn_kernel, blk=blk, hb=hb),
        grid=(batch, nh, nq),
        in_specs=[
            pl.BlockSpec((hb, blk, SB_HEAD_DIM), lambda b, h, i: (h, b * nq + i, 0)),
            pl.BlockSpec((hb, seq, SB_HEAD_DIM), lambda b, h, i: (nh + h, b, 0)),
            pl.BlockSpec((hb, seq, SB_HEAD_DIM), lambda b, h, i: (2 * nh + h, b, 0)),
            pl.BlockSpec((blk, blk + LANES), lambda b, h, i: (0, 0)),
        ],
        out_specs=pl.BlockSpec((blk, hb * SB_HEAD_DIM), lambda b, h, i: (b * nq + i, h)),
        out_shape=jax.ShapeDtypeStruct((m, heads * SB_HEAD_DIM), BF16),
        scratch_shapes=[pltpu.VMEM((hb, blk, SB_HEAD_DIM), F32), pltpu.VMEM((hb, blk, LANES), F32)],
        compiler_params=_params("parallel", "parallel", "arbitrary"),
        name="sb_attn",
    )(qkv, qkv, qkv, u)


def _sgu_out_kernel(u_ref, v_ref, vfull_ref, lng_ref, lnb_ref, ws_ref, bs_ref, w_ref, res_ref, g_ref,
                    o_ref, acc_ref, mu_ref, rstd_ref, p_ref):
    k = pl.program_id(1)
    tm, tk = u_ref.shape
    gw = tk // ws_ref.shape[0]

    @pl.when(k == 0)
    def _():
        def stats(rs, rc):
            rows = pl.ds(rs, rc)
            v = vfull_ref[rows, :].astype(F32)
            mu = jnp.mean(v, axis=-1, keepdims=True)
            vc = v - mu
            mu_ref[rows, :] = mu
            rstd_ref[rows, :] = lax.rsqrt(jnp.mean(vc * vc, axis=-1, keepdims=True) + EPS)

        _rowwise(tm, stats)

    row = lax.broadcasted_iota(jnp.int32, (SG_CHUNK, SG_CHUNK), 0)
    col = lax.broadcasted_iota(jnp.int32, (SG_CHUNK, SG_CHUNK), 1)
    for gi in range(ws_ref.shape[0]):
        cols = slice(gi * gw, (gi + 1) * gw)
        ws = jnp.where(col <= row, ws_ref[gi], 0.0).astype(BF16)
        bias = jnp.concatenate([bs_ref[gi]] * (gw // LANES), axis=1)
        for c in range(tm // SG_CHUNK):
            rows = slice(c * SG_CHUNK, (c + 1) * SG_CHUNK)
            vn = (v_ref[rows, cols].astype(F32) - mu_ref[rows, :]) * rstd_ref[rows, :]
            vn = (vn * lng_ref[:, cols] + lnb_ref[:, cols]).astype(BF16)
            mixed = _dot(ws, vn) + bias
            p_ref[rows, cols] = (u_ref[rows, cols].astype(F32) * mixed).astype(BF16)

    @pl.when(k == 0)
    def _():
        acc_ref[...] = _dot(p_ref[...], w_ref[...])

    @pl.when(k > 0)
    def _():
        acc_ref[...] += _dot(p_ref[...], w_ref[...])

    @pl.when(k == pl.num_programs(1) - 1)
    def _():
        _finish_residual(acc_ref, res_ref, g_ref, o_ref)


def _sgu_out(uv, ln_g, ln_b, w_spatial, b_spatial_lanes, w_out, res, g, seq):
    m = uv.shape[0]
    width = uv.shape[1] // 2
    d = w_out.shape[1]
    gw = width // SG_GROUPS
    tm = _pick(seq, (512, 256, 128))
    tk = _pick(width, (512, 256, 128))
    tk = max(tk, gw)
    gps = tk // gw
    nk = width // tk
    return pl.pallas_call(
        _sgu_out_kernel,
        grid=(m // tm, nk),
        in_specs=[
            pl.BlockSpec((tm, tk), lambda i, k: (i, k)),
            pl.BlockSpec((tm, tk), lambda i, k: (i, k + nk)),
            pl.BlockSpec((tm, width), lambda i, k: (i, 1)),
            pl.BlockSpec((1, tk), lambda i, k: (0, k)),
            pl.BlockSpec((1, tk), lambda i, k: (0, k)),
            pl.BlockSpec((gps, SG_CHUNK, SG_CHUNK), lambda i, k: (k, 0, 0)),
            pl.BlockSpec((gps, SG_CHUNK, LANES), lambda i, k: (k, 0, 0)),
            pl.BlockSpec((tk, d), lambda i, k: (k, 0)),
            pl.BlockSpec((tm, d), lambda i, k: (i, 0)),
            pl.BlockSpec((1, d), lambda i, k: (0, 0)),
        ],
        out_specs=pl.BlockSpec((tm, d), lambda i, k: (i, 0)),
        out_shape=jax.ShapeDtypeStruct((m, d), F32),
        scratch_shapes=[
            pltpu.VMEM((tm, d), F32),
            pltpu.VMEM((tm, 1), F32),
            pltpu.VMEM((tm, 1), F32),
            pltpu.VMEM((tm, tk), BF16),
        ],
        compiler_params=_params("parallel", "arbitrary"),
        name="sgu_out",
    )(uv, uv, uv, ln_g, ln_b, w_spatial, b_spatial_lanes, w_out, res, g)


def _exact_tril_matmul(tril, a):
    a1 = a.astype(BF16)
    r1 = a - a1.astype(F32)
    a2 = r1.astype(BF16)
    a3 = (r1 - a2.astype(F32)).astype(BF16)
    return _dot(tril, a1) + _dot(tril, a2) + _dot(tril, a3)


def _ssd_scan_kernel(z_ref, xs_ref, bc_ref, dt_ref, a_ref, dskip_ref, ng_ref, tril_ref,
                     o_ref, state_ref, y_ref, *, heads_per_group):
    q = SSD_CHUNK
    n = SSD_STATE
    p = SSD_HEAD_DIM
    groups = SSD_GROUPS
    pairs = heads_per_group // 2
    gwidth = heads_per_group * p

    @pl.when(pl.program_id(1) == 0)
    def _():
        state_ref[...] = jnp.zeros_like(state_ref)

    dt = dt_ref[...]
    a_cum = _exact_tril_matmul(tril_ref[...], dt * a_ref[...])
    a_cum_t = a_cum.T
    dt_t = dt.T
    a_last = jnp.broadcast_to(a_cum_t[:, q - 1:q], (LANES, q))
    w_t = jnp.exp(a_last - a_cum_t) * dt_t
    chunk_decay = jnp.exp(a_last)

    row = lax.broadcasted_iota(jnp.int32, (q, q), 0)
    col = lax.broadcasted_iota(jnp.int32, (q, q), 1)
    causal = col <= row
    low_half = lax.broadcasted_iota(jnp.int32, (q, LANES), 1) < p

    for g in range(groups):
        bg = bc_ref[:, g * n:(g + 1) * n]
        cg = bc_ref[:, (groups + g) * n:(groups + g + 1) * n]
        cb = _dot_nt(cg, bg)
        bg_t = bg.astype(F32).T
        y_off = _dot(cg, state_ref[g].astype(BF16))
        for pr in range(pairs):
            h0 = g * heads_per_group + 2 * pr
            lanes = slice(g * gwidth + pr * LANES, g * gwidth + (pr + 1) * LANES)
            glanes = slice(pr * LANES, (pr + 1) * LANES)
            top, bottom, decay_in = [], [], []
            for h in (h0, h0 + 1):
                along_rows = jnp.broadcast_to(a_cum[:, h:h + 1], (q, q))
                along_cols = jnp.broadcast_to(a_cum_t[h:h + 1, :], (q, q))
                decay = jnp.where(causal, jnp.exp(along_rows - along_cols), 0.0)
                top.append((cb * decay * dt_t[h:h + 1, :]).astype(BF16))
                bottom.append((bg_t * w_t[h:h + 1, :]).astype(BF16))
                decay_in.append(along_rows)
            lhs = jnp.concatenate([jnp.concatenate(top, axis=1), jnp.concatenate(bottom, axis=1)], axis=0)
            xp = xs_ref[:, lanes]
            zero = jnp.zeros_like(xp)
            rhs = jnp.concatenate([jnp.where(low_half, xp, zero), jnp.where(low_half, zero, xp)], axis=0)
            res = _dot(lhs, rhs)
            enter = jnp.exp(jnp.where(low_half, decay_in[0], decay_in[1]))
            y_ref[:, lanes] = res[:q] + y_off[:, glanes] * enter + xp.astype(F32) * dskip_ref[:, lanes]
            carry = jnp.where(low_half[:n], chunk_decay[h0:h0 + 1, :], chunk_decay[h0 + 1:h0 + 2, :])
            state_ref[g, :, glanes] = state_ref[g, :, glanes] * carry + res[q:]

    y = y_ref[...] * _silu(z_ref[...].astype(F32))
    o_ref[...] = _rms(y, ng_ref[...]).astype(o_ref.dtype)


def _ssd_scan(z, xbc, dt, a_row, dskip, norm_g, batch, seq):
    m, d_inner = z.shape
    q = SSD_CHUNK
    nc = seq // q
    heads = d_inner // SSD_HEAD_DIM
    hpg = heads // SSD_GROUPS
    bc_width = 2 * SSD_GROUPS * SSD_STATE
    assert SSD_HEAD_DIM * 2 == LANES and hpg % 2 == 0 and d_inner % bc_width == 0
    tril = (lax.broadcasted_iota(jnp.int32, (q, q), 1) <= lax.broadcasted_iota(jnp.int32, (q, q), 0)).astype(BF16)
    rows = lambda b, c: b * nc + c
    return pl.pallas_call(
        functools.partial(_ssd_scan_kernel, heads_per_group=hpg),
        grid=(batch, nc),
        in_specs=[
            pl.BlockSpec((q, d_inner), lambda b, c: (rows(b, c), 0)),
            pl.BlockSpec((q, d_inner), lambda b, c: (rows(b, c), 0)),
            pl.BlockSpec((q, bc_width), lambda b, c: (rows(b, c), d_inner // bc_width)),
            pl.BlockSpec((q, LANES), lambda b, c: (rows(b, c), 0)),
            pl.BlockSpec((1, LANES), lambda b, c: (0, 0)),
            pl.BlockSpec((1, d_inner), lambda b, c: (0, 0)),
            pl.BlockSpec((1, d_inner), lambda b, c: (0, 0)),
            pl.BlockSpec((q, q), lambda b, c: (0, 0)),
        ],
        out_specs=pl.BlockSpec((q, d_inner), lambda b, c: (rows(b, c), 0)),
        out_shape=jax.ShapeDtypeStruct((m, d_inner), BF16),
        scratch_shapes=[
            pltpu.VMEM((SSD_GROUPS, SSD_STATE, hpg * SSD_HEAD_DIM), F32),
            pltpu.VMEM((q, d_inner), F32),
        ],
        compiler_params=_params("parallel", "arbitrary"),
        name="ssd_scan",
    )(z, xbc, xbc, dt, a_row, dskip, norm_g, tril)


def _row(v):
    return v.reshape(1, -1).astype(F32)


def _pad_lanes(v):
    return jnp.pad(v, [(0, 0)] * (v.ndim - 1) + [(0, LANES - v.shape[-1])])


def _ssd_mixer(x, g_pre, w_in, conv_w, conv_b, dt_bias, a_log, d_skip, norm_g, w_out, g_post, batch, seq):
    d_inner = w_out.shape[0]
    heads = d_inner // SSD_HEAD_DIM
    n_main = w_in.shape[1] - heads
    w_z = w_in[:, :d_inner].astype(BF16)
    w_xbc = w_in[:, d_inner:n_main].astype(BF16)
    w_dt = _pad_lanes(w_in[:, n_main:]).astype(BF16)
    z = _norm_matmul(x, g_pre, w_z, jnp.ones((1, d_inner), F32))
    xbc, dt = _conv_proj(x, g_pre, w_xbc, conv_w, _row(conv_b), seq, streams=1,
                         dt_proj=(w_dt, _pad_lanes(_row(dt_bias))), name="ssd_in")
    a_row = _pad_lanes(_row(-jnp.exp(a_log.astype(F32))))
    dskip = _row(jnp.repeat(d_skip.astype(F32), SSD_HEAD_DIM))
    y = _ssd_scan(z, xbc, dt, a_row, dskip, _row(norm_g), batch, seq)
    return _matmul_norm_res(y, w_out.astype(BF16), x, g_post)


def _sgu_mixer(x, g_pre, w_in, ln_g, ln_b, w_spatial, b_spatial, w_out, g_post, seq):
    n = w_in.shape[1]
    uv = _norm_matmul(x, g_pre, w_in.astype(BF16), jnp.ones((1, n), F32), act="gelu")
    b_lanes = jnp.broadcast_to(b_spatial.astype(F32)[:, :, None], b_spatial.shape + (LANES,))
    return _sgu_out(uv, _row(ln_g), _row(ln_b), w_spatial.astype(F32), b_lanes, w_out.astype(BF16), x, g_post, seq)


def _sb_mixer(x, g_pre, w_qkv, w_out, g_post, batch, seq):
    d = x.shape[1]
    colscale = jnp.concatenate([jnp.full((1, d), SB_HEAD_DIM ** -0.5, F32), jnp.ones((1, 2 * d), F32)], axis=1)
    qkv = _norm_matmul(x, g_pre, w_qkv.astype(BF16), colscale, head_major=True)
    o = _sb_attn(qkv, batch, seq)
    return _matmul_norm_res(o, w_out.astype(BF16), x, g_post)


def kernel(x, mem, ln_mix_pre, ln_mix_post, ln_mem, ln_xa_pre, ln_xa_post, ln_ffn_pre, ln_ffn_post, xa_wq, xa_wkv, xa_wo, ffn_w_in, ffn_conv_w, ffn_conv_b, ffn_w_out, ssd_w_in, ssd_conv_w, ssd_conv_b, ssd_dt_bias, ssd_a_log, ssd_d, ssd_norm, ssd_w_out, sg_w_in, sg_v_norm_g, sg_v_norm_b, sg_w_spatial, sg_b_spatial, sg_w_out, sb_w_qkv, sb_w_out):
    batch, seq, d = x.shape
    mem_len = mem.shape[1]
    depth = ln_mix_pre.shape[0]
    h = x.reshape(batch * seq, d).astype(F32)
    mem2d = mem.reshape(batch * mem_len, d).astype(F32)
    for i in range(depth):
        kind, j = i % 3, i // 3
        g_pre, g_post = _row(ln_mix_pre[i]), _row(ln_mix_post[i])
        if kind == 0:
            h = _ssd_mixer(h, g_pre, ssd_w_in[j], ssd_conv_w[j].astype(F32), ssd_conv_b[j], ssd_dt_bias[j],
                           ssd_a_log[j], ssd_d[j], ssd_norm[j], ssd_w_out[j], g_post, batch, seq)
        elif kind == 1:
            h = _sgu_mixer(h, g_pre, sg_w_in[j], sg_v_norm_g[j], sg_v_norm_b[j], sg_w_spatial[j],
                           sg_b_spatial[j], sg_w_out[j], g_post, seq)
        else:
            h = _sb_mixer(h, g_pre, sb_w_qkv[j], sb_w_out[j], g_post, batch, seq)
        kv = _mem_kv(mem2d, _row(ln_mem[i]), xa_wkv[i].astype(BF16), mem_len)
        h = _xattn(h, _row(ln_xa_pre[i]), xa_wq[i].astype(BF16), kv, xa_wo[i].astype(BF16),
                   _row(ln_xa_post[i]), seq, mem_len)
        f = _conv_proj(h, _row(ln_ffn_pre[i]), ffn_w_in[i].astype(BF16), ffn_conv_w[i].astype(F32),
                       _row(ffn_conv_b[i]), seq, streams=2, name="ffn_in")
        h = _matmul_norm_res(f, ffn_w_out[i].astype(BF16), h, _row(ln_ffn_post[i]))
    return h.reshape(batch, seq, d).astype(x.dtype)
```
